```python
import math
import jax, jax.numpy as jnp
from jax import lax
import numpy as np

D_MODEL = 1024
BATCH = 4
SEQ = 8192
DEPTH = 1

CONV_CH = 768
CONV_K = 3
SSM_CH = 256
SSM_GROUP = 16
SSM_GROUPS = SSM_CH // SSM_GROUP
SSM_STATE = 64
DT_MIN = 1e-3
DT_MAX = 1e-1
N_BRANCH = 2
IN_COLS = 3 * CONV_CH + SSM_CH + N_BRANCH * D_MODEL
PEER_HEADS = 8
PEER_KEYS = 128
PEER_N = PEER_KEYS * PEER_KEYS
PEER_QDIM = 256
PEER_HALF = PEER_QDIM // 2
PEER_TOPK = 16
PEER_CHUNK = 128
RMS_EPS = 1e-6

kernel_name = "hybrid_conv_s5_peer_block"


def rmsnorm(x, g):
    xf = x.astype(jnp.float32)
    y = xf * lax.rsqrt(jnp.mean(xf * xf, axis=-1, keepdims=True) + RMS_EPS)
    return (y * g.astype(jnp.float32)).astype(x.dtype)


def short_conv_mixer(b, c, h, conv_w, conv_b):
    v = c * h
    s = v.shape[1]
    vp = jnp.pad(v, ((0, 0), (CONV_K - 1, 0), (0, 0)))
    y = conv_b
    for k in range(CONV_K):
        y = y + conv_w[k] * vp[:, k:k + s]
    return b * y


def _complex_scan_op(e1, e2):
    a1r, a1i, b1r, b1i = e1
    a2r, a2i, b2r, b2i = e2
    ar = a2r * a1r - a2i * a1i
    ai = a2r * a1i + a2i * a1r
    br = a2r * b1r - a2i * b1i + b2r
    bi = a2r * b1i + a2i * b1r + b2i
    return (ar, ai, br, bi)


def s5_mixer(u, lam_re, lam_im, log_dt, b_re, b_im, c_re, c_im, d_skip, w_glu, b_glu):
    f32 = jnp.float32
    bsz, s, _ = u.shape
    uf = u.astype(f32).reshape(bsz, s, SSM_GROUPS, SSM_GROUP)
    lr = lam_re.astype(f32)
    li = lam_im.astype(f32)
    dt = jnp.exp(log_dt.astype(f32))[:, None]
    mag = jnp.exp(lr * dt)
    ang = li * dt
    a_re = mag * jnp.cos(ang)
    a_im = mag * jnp.sin(ang)
    den = lr * lr + li * li
    nr = a_re - 1.0
    ni = a_im
    k_re = (nr * lr + ni * li) / den
    k_im = (ni * lr - nr * li) / den
    bu_re = jnp.einsum('bsgh,gph->bsgp', uf, b_re.astype(f32))
    bu_im = jnp.einsum('bsgh,gph->bsgp', uf, b_im.astype(f32))
    x_re = k_re * bu_re - k_im * bu_im
    x_im = k_re * bu_im + k_im * bu_re
    A_re = jnp.broadcast_to(a_re, x_re.shape)
    A_im = jnp.broadcast_to(a_im, x_im.shape)
    _, _, s_re, s_im = lax.associative_scan(_complex_scan_op, (A_re, A_im, x_re, x_im), axis=1)
    y = (jnp.einsum('bsgp,ghp->bsgh', s_re, c_re.astype(f32))
         - jnp.einsum('bsgp,ghp->bsgh', s_im, c_im.astype(f32))
         + d_skip.astype(f32) * uf)
    y = y.reshape(bsz, s, SSM_CH)
    z = jax.nn.gelu(y, approximate=False)
    out = z * jax.nn.sigmoid(z @ w_glu.astype(f32) + b_glu.astype(f32))
    return out.astype(u.dtype)


def peer_ffn(x, w_q, sub_k1, sub_k2, expert_u, expert_v):
    f32 = jnp.float32
    bsz, s, d = x.shape
    q = (x @ w_q).astype(f32).reshape(bsz, s, PEER_HEADS, 2, PEER_HALF)
    s1 = jnp.einsum('bshd,nd->bshn', q[..., 0, :], sub_k1.astype(f32))
    s2 = jnp.einsum('bshd,nd->bshn', q[..., 1, :], sub_k2.astype(f32))
    v1, i1 = lax.top_k(s1, PEER_TOPK)
    v2, i2 = lax.top_k(s2, PEER_TOPK)
    cand = (v1[..., :, None] + v2[..., None, :]).reshape(bsz, s, PEER_HEADS, PEER_TOPK * PEER_TOPK)
    cidx = (i1[..., :, None] * PEER_KEYS + i2[..., None, :]).reshape(bsz, s, PEER_HEADS, PEER_TOPK * PEER_TOPK)
    best, pos = lax.top_k(cand, PEER_TOPK)
    eidx = jnp.take_along_axis(cidx, pos, axis=-1)
    gate = jax.nn.softmax(best, axis=-1)
    n_chunks = (bsz * s) // PEER_CHUNK
    hk = PEER_HEADS * PEER_TOPK
    xs = x.reshape(n_chunks, PEER_CHUNK, d)
    es = eidx.reshape(n_chunks, PEER_CHUNK, hk)
    gs = gate.reshape(n_chunks, PEER_CHUNK, hk).astype(x.dtype)

    def block(args):
        xc, ec, gc = args
        u = jnp.take(expert_u, ec, axis=0)
        v = jnp.take(expert_v, ec, axis=0)
        h = jax.nn.gelu(jnp.einsum('cd,ckd->ck', xc, u), approximate=False)
        return jnp.einsum('ck,ckd->cd', gc * h, v)

    out = lax.map(block, (xs, es, gs))
    return out.reshape(bsz, s, d)


def setup_inputs(seed: int = 0) -> dict:
    key = jax.random.key(seed)
    ks = jax.random.split(key, 32)
    f32 = jnp.float32
    L = DEPTH
    D = D_MODEL

    def nrm(k, shape, scale):
        return jax.random.normal(k, shape, f32) * scale

    n_idx = jnp.arange(SSM_STATE, dtype=f32)
    lam_re = -0.5 + nrm(ks[6], (L, SSM_GROUPS, SSM_STATE), 0.01)
    lam_im = math.pi * n_idx[None, None, :] + nrm(ks[7], (L, SSM_GROUPS, SSM_STATE), 0.01)
    log_dt = jax.random.uniform(ks[8], (L, SSM_GROUPS), f32, math.log(DT_MIN), math.log(DT_MAX))
    return {
        "x": nrm(ks[0], (BATCH, SEQ, D), 1.0),
        "g_mix": 1.0 + nrm(ks[1], (L, D), 0.01),
        "w_in": nrm(ks[2], (L, D, IN_COLS), D ** -0.5),
        "b_gate": nrm(ks[3], (L, N_BRANCH * D), 0.01),
        "conv_w": nrm(ks[4], (L, CONV_K, CONV_CH), CONV_K ** -0.5),
        "conv_b": nrm(ks[5], (L, CONV_CH), 0.01),
        "lam_re": lam_re,
        "lam_im": lam_im,
        "log_dt": log_dt,
        "ssm_b_re": nrm(ks[9], (L, SSM_GROUPS, SSM_STATE, SSM_GROUP), (2 * SSM_GROUP) ** -0.5),
        "ssm_b_im": nrm(ks[10], (L, SSM_GROUPS, SSM_STATE, SSM_GROUP), (2 * SSM_GROUP) ** -0.5),
        "ssm_c_re": nrm(ks[11], (L, SSM_GROUPS, SSM_GROUP, SSM_STATE), SSM_STATE ** -0.5),
        "ssm_c_im": nrm(ks[12], (L, SSM_GROUPS, SSM_GROUP, SSM_STATE), SSM_STATE ** -0.5),
        "ssm_d": nrm(ks[13], (L, SSM_GROUPS, SSM_GROUP), 1.0),
        "w_glu": nrm(ks[14], (L, SSM_CH, SSM_CH), SSM_CH ** -0.5),
        "b_glu": nrm(ks[15], (L, SSM_CH), 0.01),
        "w_branch_a": nrm(ks[16], (L, CONV_CH, D), CONV_CH ** -0.5),
        "w_branch_b": nrm(ks[17], (L, SSM_CH, D), SSM_CH ** -0.5),
        "w_out": nrm(ks[18], (L, D, D), D ** -0.5),
        "g_ffn": 1.0 + nrm(ks[19], (L, D), 0.01),
        "w_q": nrm(ks[20], (L, D, PEER_HEADS * PEER_QDIM), D ** -0.5),
        "sub_keys_1": nrm(ks[21], (L, PEER_KEYS, PEER_HALF), PEER_HALF ** -0.5),
        "sub_keys_2": nrm(ks[22], (L, PEER_KEYS, PEER_HALF), PEER_HALF ** -0.5),
        "expert_u": nrm(ks[23], (L, PEER_N, D), D ** -0.5),
        "expert_v": nrm(ks[24], (L, PEER_N, D), PEER_HEADS ** -0.5),
        "g_final": 1.0 + nrm(ks[25], (D,), 0.01),
    }


def reference(x, g_mix, w_in, b_gate, conv_w, conv_b, lam_re, lam_im, log_dt,
              ssm_b_re, ssm_b_im, ssm_c_re, ssm_c_im, ssm_d, w_glu, b_glu,
              w_branch_a, w_branch_b, w_out, g_ffn, w_q, sub_keys_1, sub_keys_2,
              expert_u, expert_v, g_final):
    bsz, s, d = x.shape
    splits = [CONV_CH, 2 * CONV_CH, 3 * CONV_CH, 3 * CONV_CH + SSM_CH]
    for l in range(DEPTH):
        h = rmsnorm(x, g_mix[l])
        proj = h @ w_in[l]
        cb, cc, cx, su, gl = jnp.split(proj, splits, axis=-1)
        ya = short_conv_mixer(cb, cc, cx, conv_w[l], conv_b[l]) @ w_branch_a[l]
        yb = s5_mixer(su, lam_re[l], lam_im[l], log_dt[l], ssm_b_re[l], ssm_b_im[l],
                      ssm_c_re[l], ssm_c_im[l], ssm_d[l], w_glu[l], b_glu[l]) @ w_branch_b[l]
        g = jax.nn.sigmoid(gl + b_gate[l]).reshape(bsz, s, N_BRANCH, d)
        merged = g[:, :, 0] * ya + g[:, :, 1] * yb
        x = x + merged @ w_out[l]
        x = x + peer_ffn(rmsnorm(x, g_ffn[l]), w_q[l], sub_keys_1[l], sub_keys_2[l],
                         expert_u[l], expert_v[l])
    return rmsnorm(x, g_final)
```

```python
import functools
import math

import jax
import jax.numpy as jnp
from jax import lax
from jax.experimental import pallas as pl
from jax.experimental.pallas import tpu as pltpu

D_MODEL = 1024
CONV_CH = 768
CONV_K = 3
SSM_CH = 256
SSM_GROUP = 16
SSM_GROUPS = SSM_CH // SSM_GROUP
SSM_STATE = 64
N_BRANCH = 2
PEER_HEADS = 8
PEER_KEYS = 128
PEER_QDIM = 256
PEER_HALF = PEER_QDIM // 2
PEER_TOPK = 16
PEER_HK = PEER_HEADS * PEER_TOPK
RMS_EPS = 1e-6

LANES = 128
PEER_CHUNK = 128
PEER_GROUP = 8
PEER_SLOTS = 2


def _gelu(x):
    return 0.5 * x * (1.0 + lax.erf(x * (2.0 ** -0.5)))


def _peer_kernel(eidx_ref, xn_ref, gt_ref, uv_hbm, out_ref, buf, sem):
    i = pl.program_id(0)
    n_groups = PEER_CHUNK // PEER_GROUP
    dc = D_MODEL // LANES
    lane_id = lax.broadcasted_iota(jnp.int32, (PEER_HK, LANES), 1)

    def issue_token(row, slot, j):
        for k in range(PEER_HK):
            idx = eidx_ref[row, k]
            pltpu.make_async_copy(
                uv_hbm.at[idx],
                buf.at[slot, :, j * PEER_HK + k, :],
                sem.at[slot]).start()

    def wait_slot(slot):
        pltpu.make_async_copy(buf.at[slot], buf.at[slot], sem.at[slot]).wait()

    @pl.when(i == 0)
    def _():
        for j in range(PEER_GROUP):
            issue_token(j, 0, j)

    def group_body(grp, carry):
        slot = grp % PEER_SLOTS
        nslot = (grp + 1) % PEER_SLOTS
        base = pl.multiple_of(grp * PEER_GROUP, PEER_GROUP)
        wait_slot(slot)
        xg = xn_ref[pl.ds(base, PEER_GROUP), :]

        hmat = jnp.zeros((PEER_HK, LANES), jnp.float32)
        for j in range(PEER_GROUP):
            issue_token(base + PEER_GROUP + j, nslot, j)
            acc = jnp.zeros((PEER_HK, LANES), jnp.float32)
            for c in range(dc):
                u = buf[slot, c, j * PEER_HK:(j + 1) * PEER_HK, :]
                acc = acc + u * xg[j:j + 1, c * LANES:(c + 1) * LANES]
            s = jnp.sum(acc, axis=1, keepdims=True)
            hmat = jnp.where(lane_id == base + j, s, hmat)

        wmat = _gelu(hmat) * gt_ref[...]

        orows = []
        for j in range(PEER_GROUP):
            w = jnp.sum(jnp.where(lane_id == base + j, wmat, 0.0), axis=1, keepdims=True)
            cols = []
            for c in range(dc):
                v = buf[slot, dc + c, j * PEER_HK:(j + 1) * PEER_HK, :]
                cols.append(jnp.sum(v * w, axis=0, keepdims=True))
            orows.append(jnp.concatenate(cols, axis=1))
        out_ref[pl.ds(base, PEER_GROUP), :] = jnp.concatenate(orows, axis=0)
        return carry

    lax.fori_loop(0, n_groups, group_body, 0)

    @pl.when(i == pl.num_programs(0) - 1)
    def _():
        wait_slot(n_groups % PEER_SLOTS)


def _peer_experts(xn, eidx, gate_t, uv):
    t = xn.shape[0]
    n_chunks = t // PEER_CHUNK
    dc = D_MODEL // LANES
    e3 = eidx.reshape(n_chunks, PEER_CHUNK, PEER_HK)
    e_ext = jnp.concatenate([e3, jnp.roll(e3, -1, axis=0)[:, :PEER_GROUP]], axis=1)
    uv3 = uv.reshape(uv.shape[0], 2 * dc, LANES)
    buf_bytes = PEER_SLOTS * PEER_GROUP * PEER_HK * 2 * D_MODEL * 4
    return pl.pallas_call(
        _peer_kernel,
        grid=(n_chunks,),
        in_specs=[
            pl.BlockSpec((None, PEER_CHUNK + PEER_GROUP, PEER_HK), lambda i: (i, 0, 0),
                         memory_space=pltpu.SMEM),
            pl.BlockSpec((PEER_CHUNK, D_MODEL), lambda i: (i, 0)),
            pl.BlockSpec((PEER_HK, PEER_CHUNK), lambda i: (0, i)),
            pl.BlockSpec(memory_space=pl.ANY),
        ],
        out_specs=pl.BlockSpec((PEER_CHUNK, D_MODEL), lambda i: (i, 0)),
        out_shape=jax.ShapeDtypeStruct((t, D_MODEL), jnp.float32),
        scratch_shapes=[
            pltpu.VMEM((PEER_SLOTS, 2 * dc, PEER_GROUP * PEER_HK, LANES), jnp.float32),
            pltpu.SemaphoreType.DMA((PEER_SLOTS,)),
        ],
        compiler_params=pltpu.CompilerParams(
            dimension_semantics=("arbitrary",),
            vmem_limit_bytes=buf_bytes + (8 << 20)),
        name="peer_experts",
    )(e_ext, xn, gate_t, uv3)


def _rmsnorm(x, g):
    return x * lax.rsqrt(jnp.mean(x * x, axis=-1, keepdims=True) + RMS_EPS) * g


def _short_conv(b, c, h, conv_w, conv_b):
    v = c * h
    s = v.shape[1]
    vp = jnp.pad(v, ((0, 0), (CONV_K - 1, 0), (0, 0)))
    y = conv_b
    for k in range(CONV_K):
        y = y + conv_w[k] * vp[:, k:k + s]
    return b * y


def _scan_op(e1, e2):
    a1r, a1i, b1r, b1i = e1
    a2r, a2i, b2r, b2i = e2
    return (a2r * a1r - a2i * a1i, a2r * a1i + a2i * a1r,
            a2r * b1r - a2i * b1i + b2r, a2r * b1i + a2i * b1r + b2i)


def _s5(u, lam_re, lam_im, log_dt, b_re, b_im, c_re, c_im, d_skip, w_glu, b_glu):
    bsz, s, _ = u.shape
    uf = u.reshape(bsz, s, SSM_GROUPS, SSM_GROUP)
    dt = jnp.exp(log_dt)[:, None]
    mag = jnp.exp(lam_re * dt)
    ang = lam_im * dt
    a_re = mag * jnp.cos(ang)
    a_im = mag * jnp.sin(ang)
    den = lam_re * lam_re + lam_im * lam_im
    nr = a_re - 1.0
    ni = a_im
    k_re = (nr * lam_re + ni * lam_im) / den
    k_im = (ni * lam_re - nr * lam_im) / den
    bu_re = jnp.einsum('bsgh,gph->bsgp', uf, b_re)
    bu_im = jnp.einsum('bsgh,gph->bsgp', uf, b_im)
    x_re = k_re * bu_re - k_im * bu_im
    x_im = k_re * bu_im + k_im * bu_re
    big_a_re = jnp.broadcast_to(a_re, x_re.shape)
    big_a_im = jnp.broadcast_to(a_im, x_im.shape)
    _, _, s_re, s_im = lax.associative_scan(_scan_op, (big_a_re, big_a_im, x_re, x_im), axis=1)
    y = (jnp.einsum('bsgp,ghp->bsgh', s_re, c_re)
         - jnp.einsum('bsgp,ghp->bsgh', s_im, c_im)
         + d_skip * uf)
    y = y.reshape(bsz, s, SSM_CH)
    z = jax.nn.gelu(y, approximate=False)
    return z * jax.nn.sigmoid(z @ w_glu + b_glu)


def _peer_route(xn, w_q, sub_k1, sub_k2):
    bsz, s, _ = xn.shape
    q = (xn @ w_q).reshape(bsz, s, PEER_HEADS, 2, PEER_HALF)
    s1 = jnp.einsum('bshd,nd->bshn', q[..., 0, :], sub_k1)
    s2 = jnp.einsum('bshd,nd->bshn', q[..., 1, :], sub_k2)
    v1, i1 = lax.top_k(s1, PEER_TOPK)
    v2, i2 = lax.top_k(s2, PEER_TOPK)
    cand = (v1[..., :, None] + v2[..., None, :]).reshape(bsz, s, PEER_HEADS, PEER_TOPK * PEER_TOPK)
    cidx = (i1[..., :, None] * PEER_KEYS + i2[..., None, :]).reshape(bsz, s, PEER_HEADS, PEER_TOPK * PEER_TOPK)
    best, pos = lax.top_k(cand, PEER_TOPK)
    eidx = jnp.take_along_axis(cidx, pos, axis=-1)
    gate = jax.nn.softmax(best, axis=-1)
    return eidx.reshape(bsz * s, PEER_HK), gate.reshape(bsz * s, PEER_HK)


def kernel(x, g_mix, w_in, b_gate, conv_w, conv_b, lam_re, lam_im, log_dt, ssm_b_re, ssm_b_im, ssm_c_re, ssm_c_im, ssm_d, w_glu, b_glu, w_branch_a, w_branch_b, w_out, g_ffn, w_q, sub_keys_1, sub_keys_2, expert_u, expert_v, g_final):
    bsz, s, d = x.shape
    depth = g_mix.shape[0]
    splits = [CONV_CH, 2 * CONV_CH, 3 * CONV_CH, 3 * CONV_CH + SSM_CH]
    for l in range(depth):
        h = _rmsnorm(x, g_mix[l])
        proj = h @ w_in[l]
        cb, cc, cx, su, gl = jnp.split(proj, splits, axis=-1)
        ya = _short_conv(cb, cc, cx, conv_w[l], conv_b[l]) @ w_branch_a[l]
        yb = _s5(su, lam_re[l], lam_im[l], log_dt[l], ssm_b_re[l], ssm_b_im[l],
                 ssm_c_re[l], ssm_c_im[l], ssm_d[l], w_glu[l], b_glu[l]) @ w_branch_b[l]
        g = jax.nn.sigmoid(gl + b_gate[l]).reshape(bsz, s, N_BRANCH, d)
        merged = g[:, :, 0] * ya + g[:, :, 1] * yb
        x = x + merged @ w_out[l]
        xn = _rmsnorm(x, g_ffn[l])
        eidx, gate = _peer_route(xn, w_q[l], sub_keys_1[l], sub_keys_2[l])
        uv = jnp.concatenate([expert_u[l], expert_v[l]], axis=1)
        y = _peer_experts(xn.reshape(bsz * s, d), eidx.astype(jnp.int32), gate.T, uv)
        x = x + y.reshape(bsz, s, d)
    return _rmsnorm(x, g_final)
```

```python
import functools

import jax
import jax.numpy as jnp
from jax import lax
from jax.experimental import pallas as pl
from jax.experimental.pallas import tpu as pltpu

D_MODEL = 1024
CONV_CH = 768
CONV_K = 3
SSM_CH = 256
SSM_GROUP = 16
SSM_GROUPS = SSM_CH // SSM_GROUP
SSM_STATE = 64
SSM_N = SSM_GROUPS * SSM_STATE
N_BRANCH = 2
IN_COLS = 3 * CONV_CH + SSM_CH + N_BRANCH * D_MODEL
PEER_HEADS = 8
PEER_KEYS = 128
PEER_QDIM = 256
PEER_HALF = PEER_QDIM // 2
PEER_TOPK = 16
PEER_HK = PEER_HEADS * PEER_TOPK
RMS_EPS = 1e-6

LANES = 128
SUBLANES = 8
BF16 = jnp.bfloat16

MIX_TM = 512
S5_TS = 256
ROUTE_TM = 256
PEER_CHUNK = 128
PEER_GROUP = SUBLANES
PEER_SLOTS = 4
PEER_AHEAD = PEER_SLOTS - 1


def _gelu(x):
    return 0.5 * x * (1.0 + lax.erf(x * (2.0 ** -0.5)))


def _sigmoid(x):
    return 1.0 / (1.0 + jnp.exp(-x))


def _rms(x, g):
    return x * lax.rsqrt(jnp.mean(x * x, axis=-1, keepdims=True) + RMS_EPS) * g


def _resident(shape):
    return pl.BlockSpec(shape, lambda *_: (0,) * len(shape), pipeline_mode=pl.Buffered(1))


def _mix_in_kernel(tiles_per_seq, x_ref, g_ref, win_ref, bg_ref, cw_ref, cb_ref, wa_ref,
                   yag_ref, gb_ref, su_ref, carry):
    i = pl.program_id(0)
    tm = x_ref.shape[0]
    h = _rms(x_ref[...], g_ref[...])
    proj = jnp.dot(h.astype(BF16), win_ref[...], preferred_element_type=jnp.float32)
    c0, c1, c2, c3 = CONV_CH, 2 * CONV_CH, 3 * CONV_CH, 3 * CONV_CH + SSM_CH
    cb = proj[:, :c0]
    v = proj[:, c0:c1] * proj[:, c1:c2]
    su_ref[...] = proj[:, c2:c3]

    @pl.when(i % tiles_per_seq == 0)
    def _():
        carry[...] = jnp.zeros_like(carry)

    prev = carry[...]
    row = lax.broadcasted_iota(jnp.int32, (tm, CONV_CH), 0)
    vm1 = jnp.where(row == 0, prev[SUBLANES - 1:, :], pltpu.roll(v, 1, axis=0))
    vm2 = jnp.where(row == 0, prev[SUBLANES - 2:SUBLANES - 1, :],
                    jnp.where(row == 1, prev[SUBLANES - 1:, :], pltpu.roll(v, 2, axis=0)))
    carry[...] = v[tm - SUBLANES:, :]
    cw = cw_ref[...]
    y = cb_ref[...] + cw[0:1, :] * vm2 + cw[1:2, :] * vm1 + cw[2:3, :] * v
    ya = jnp.dot((cb * y).astype(BF16), wa_ref[...], preferred_element_type=jnp.float32)
    g = _sigmoid(proj[:, c3:] + bg_ref[...])
    yag_ref[...] = g[:, :D_MODEL] * ya
    gb_ref[...] = g[:, D_MODEL:]


def _mix_in(x2d, seq_len, g_mix, w_in, b_gate, conv_w, conv_b, w_a):
    t = x2d.shape[0]
    tm = MIX_TM
    tok = lambda n: pl.BlockSpec((tm, n), lambda i: (i, 0))
    return pl.pallas_call(
        functools.partial(_mix_in_kernel, seq_len // tm),
        grid=(t // tm,),
        in_specs=[tok(D_MODEL), _resident((1, D_MODEL)), _resident((D_MODEL, IN_COLS)),
                  _resident((1, N_BRANCH * D_MODEL)), _resident((CONV_K, CONV_CH)),
                  _resident((1, CONV_CH)), _resident((CONV_CH, D_MODEL))],
        out_specs=[tok(D_MODEL), tok(D_MODEL), tok(SSM_CH)],
        out_shape=[jax.ShapeDtypeStruct((t, D_MODEL), jnp.float32),
                   jax.ShapeDtypeStruct((t, D_MODEL), jnp.float32),
                   jax.ShapeDtypeStruct((t, SSM_CH), jnp.float32)],
        scratch_shapes=[pltpu.VMEM((SUBLANES, CONV_CH), jnp.float32)],
        compiler_params=pltpu.CompilerParams(dimension_semantics=("arbitrary",),
                                             vmem_limit_bytes=56 << 20),
        name="mix_in",
    )(x2d, g_mix.reshape(1, -1), w_in.astype(BF16), b_gate.reshape(1, -1), conv_w,
      conv_b.reshape(1, -1), w_a.astype(BF16))


def _s5_kernel(u_ref, bre_ref, bim_ref, pw_ref, cre_ref, cim_ref, d_ref, wg_ref, bgl_ref,
               z_ref, xr_scr, xi_scr, st_scr):
    j = pl.program_id(1)
    ts = u_ref.shape[0]
    u = u_ref[...]
    ub = u.astype(BF16)
    xr_scr[...] = jnp.dot(ub, bre_ref[...], preferred_element_type=jnp.float32)
    xi_scr[...] = jnp.dot(ub, bim_ref[...], preferred_element_type=jnp.float32)

    @pl.when(j == 0)
    def _():
        st_scr[...] = jnp.zeros_like(st_scr)

    pr = pw_ref[0]
    pi = pw_ref[1]
    row = lax.broadcasted_iota(jnp.int32, (SUBLANES, SSM_N), 0)

    def blk(b, carry):
        sr, si = carry
        r0 = pl.multiple_of(b * SUBLANES, SUBLANES)
        xr = xr_scr[pl.ds(r0, SUBLANES), :]
        xi = xi_scr[pl.ds(r0, SUBLANES), :]
        for sh in (1, 2, 4):
            ar = pr[sh - 1:sh, :]
            ai = pi[sh - 1:sh, :]
            rr = pltpu.roll(xr, sh, axis=0)
            ri = pltpu.roll(xi, sh, axis=0)
            keep = row >= sh
            xr, xi = (xr + jnp.where(keep, ar * rr - ai * ri, 0.0),
                      xi + jnp.where(keep, ar * ri + ai * rr, 0.0))
        xr, xi = xr + (pr * sr - pi * si), xi + (pr * si + pi * sr)
        xr_scr[pl.ds(r0, SUBLANES), :] = xr
        xi_scr[pl.ds(r0, SUBLANES), :] = xi
        return xr[SUBLANES - 1:, :], xi[SUBLANES - 1:, :]

    st = st_scr[...]
    sr, si = lax.fori_loop(0, ts // SUBLANES, blk, (st[0:1, :], st[1:2, :]))
    st_scr[0:1, :] = sr
    st_scr[1:2, :] = si
    y = (jnp.dot(xr_scr[...].astype(BF16), cre_ref[...], preferred_element_type=jnp.float32)
         - jnp.dot(xi_scr[...].astype(BF16), cim_ref[...], preferred_element_type=jnp.float32)
         + d_ref[...] * u)
    z = _gelu(y)
    gate = _sigmoid(jnp.dot(z.astype(BF16), wg_ref[...], preferred_element_type=jnp.float32)
                    + bgl_ref[...])
    z_ref[...] = z * gate


def _block_diag(m):
    g, r, c = m.shape
    eye = jnp.eye(g, dtype=m.dtype)
    return (eye[:, None, :, None] * m[:, :, None, :]).reshape(g * r, g * c)


def _s5_params(lam_re, lam_im, log_dt, b_re, b_im, c_re, c_im):
    dt = jnp.exp(log_dt)[:, None]
    mag = jnp.exp(lam_re * dt)
    ang = lam_im * dt
    a_re = mag * jnp.cos(ang)
    a_im = mag * jnp.sin(ang)
    den = lam_re * lam_re + lam_im * lam_im
    nr = a_re - 1.0
    ni = a_im
    k_re = (nr * lam_re + ni * lam_im) / den
    k_im = (ni * lam_re - nr * lam_im) / den
    bb_re = k_re[:, :, None] * b_re - k_im[:, :, None] * b_im
    bb_im = k_re[:, :, None] * b_im + k_im[:, :, None] * b_re
    bre = _block_diag(jnp.swapaxes(bb_re, 1, 2))
    bim = _block_diag(jnp.swapaxes(bb_im, 1, 2))
    cre = _block_diag(jnp.swapaxes(c_re, 1, 2))
    cim = _block_diag(jnp.swapaxes(c_im, 1, 2))
    ar = a_re.reshape(1, -1)
    ai = a_im.reshape(1, -1)
    prs, pis = [ar], [ai]
    for _ in range(SUBLANES - 1):
        prs.append(prs[-1] * ar - pis[-1] * ai)
        pis.append(prs[-2] * ai + pis[-1] * ar)
    pw = jnp.stack([jnp.concatenate(prs, 0), jnp.concatenate(pis, 0)])
    return bre, bim, pw, cre, cim


def _s5(su2d, bsz, seq_len, lam_re, lam_im, log_dt, b_re, b_im, c_re, c_im, d_skip, w_glu, b_glu):
    ts = S5_TS
    nt = seq_len // ts
    bre, bim, pw, cre, cim = _s5_params(lam_re, lam_im, log_dt, b_re, b_im, c_re, c_im)
    tile = pl.BlockSpec((ts, SSM_CH), lambda b, j: (b * nt + j, 0))
    return pl.pallas_call(
        _s5_kernel,
        grid=(bsz, nt),
        in_specs=[tile, _resident((SSM_CH, SSM_N)), _resident((SSM_CH, SSM_N)),
                  _resident((2, SUBLANES, SSM_N)), _resident((SSM_N, SSM_CH)),
                  _resident((SSM_N, SSM_CH)), _resident((1, SSM_CH)),
                  _resident((SSM_CH, SSM_CH)), _resident((1, SSM_CH))],
        out_specs=tile,
        out_shape=jax.ShapeDtypeStruct(su2d.shape, jnp.float32),
        scratch_shapes=[pltpu.VMEM((ts, SSM_N), jnp.float32), pltpu.VMEM((ts, SSM_N), jnp.float32),
                        pltpu.VMEM((SUBLANES, SSM_N), jnp.float32)],
        compiler_params=pltpu.CompilerParams(dimension_semantics=("arbitrary", "arbitrary")),
        name="s5_mixer",
    )(su2d, bre.astype(BF16), bim.astype(BF16), pw, cre.astype(BF16), cim.astype(BF16),
      d_skip.reshape(1, -1), w_glu.astype(BF16), b_glu.reshape(1, -1))


def _topk_rows(s, row_id, k):
    vals, idxs = [], []
    for _ in range(k):
        m = jnp.max(s, axis=0, keepdims=True)
        idx = jnp.min(jnp.where(s == m, row_id, s.shape[0]), axis=0, keepdims=True)
        vals.append(m)
        idxs.append(idx)
        s = jnp.where(row_id == idx, -jnp.inf, s)
    return jnp.concatenate(vals, axis=0), jnp.concatenate(idxs, axis=0)


def _route_kernel(x_ref, yag_ref, gb_ref, zb_ref, wbb_ref, wout_ref, g_ref, wq_ref, sk_ref,
                  x2_ref, xn_ref, eidx_ref, gate_ref, q_scr, v_scr, i_scr):
    tm = x_ref.shape[0]
    yb = jnp.dot(zb_ref[...].astype(BF16), wbb_ref[...], preferred_element_type=jnp.float32)
    merged = yag_ref[...] + gb_ref[...] * yb
    x2 = x_ref[...] + jnp.dot(merged.astype(BF16), wout_ref[...],
                              preferred_element_type=jnp.float32)
    x2_ref[...] = x2
    xn = _rms(x2, g_ref[...])
    xn_ref[...] = xn
    q = jnp.dot(xn.astype(BF16), wq_ref[...], preferred_element_type=jnp.float32)
    for hh in range(2 * PEER_HEADS):
        q_scr[hh] = q[:, hh * PEER_HALF:(hh + 1) * PEER_HALF].astype(BF16)

    key_id = lax.broadcasted_iota(jnp.int32, (PEER_KEYS, tm), 0)

    def level1(hh, carry):
        s = lax.dot_general(sk_ref[hh % 2], q_scr[hh], (((1,), (1,)), ((), ())),
                            preferred_element_type=jnp.float32)
        v, i = _topk_rows(s, key_id, PEER_TOPK)
        v_scr[hh] = v
        i_scr[hh] = i
        return carry

    lax.fori_loop(0, 2 * PEER_HEADS, level1, 0)

    sub8 = lax.broadcasted_iota(jnp.int32, (SUBLANES, tm), 0)

    def level2(h, carry):
        v1 = v_scr[2 * h]
        v2 = v_scr[2 * h + 1]
        i1 = i_scr[2 * h]
        i2 = i_scr[2 * h + 1]
        cands, poss, cids = [], [], []
        for a, b0 in [(0, 0), (0, SUBLANES)] + [(a, 0) for a in range(1, PEER_TOPK)]:
            nb = PEER_TOPK // (a + 1)
            val = v1[a:a + 1, :] + v2[b0:b0 + SUBLANES, :]
            if nb < b0 + SUBLANES:
                val = jnp.where(sub8 < nb - b0, val, -jnp.inf)
            cands.append(val)
            poss.append(sub8 + (a * PEER_TOPK + b0))
            cids.append(i1[a:a + 1, :] * PEER_KEYS + i2[b0:b0 + SUBLANES, :])
        cand = jnp.concatenate(cands, axis=0)
        pos = jnp.concatenate(poss, axis=0)
        cid = jnp.concatenate(cids, axis=0)
        best, eids = [], []
        for _ in range(PEER_TOPK):
            m = jnp.max(cand, axis=0, keepdims=True)
            p = jnp.min(jnp.where(cand == m, pos, PEER_TOPK * PEER_TOPK), axis=0, keepdims=True)
            hit = pos == p
            eids.append(jnp.max(jnp.where(hit, cid, -1), axis=0, keepdims=True))
            best.append(m)
            cand = jnp.where(hit, -jnp.inf, cand)
        best = jnp.concatenate(best, axis=0)
        ex = jnp.exp(best - best[0:1, :])
        r0 = pl.multiple_of(h * PEER_TOPK, PEER_TOPK)
        eidx_ref[pl.ds(r0, PEER_TOPK), :] = jnp.concatenate(eids, axis=0)
        gate_ref[pl.ds(r0, PEER_TOPK), :] = ex / jnp.sum(ex, axis=0, keepdims=True)
        return carry

    lax.fori_loop(0, PEER_HEADS, level2, 0)


def _peer_route(x2d, yag, gb, zb, w_bb, w_out, g_ffn, w_q, sk1, sk2):
    t = x2d.shape[0]
    tm = ROUTE_TM
    tok = lambda n: pl.BlockSpec((tm, n), lambda i: (i, 0))
    tok_t = pl.BlockSpec((PEER_HK, tm), lambda i: (0, i))
    return pl.pallas_call(
        _route_kernel,
        grid=(t // tm,),
        in_specs=[tok(D_MODEL), tok(D_MODEL), tok(D_MODEL), tok(SSM_CH),
                  _resident((SSM_CH, D_MODEL)), _resident((D_MODEL, D_MODEL)),
                  _resident((1, D_MODEL)), _resident((D_MODEL, PEER_HEADS * PEER_QDIM)),
                  _resident((2, PEER_KEYS, PEER_HALF))],
        out_specs=[tok(D_MODEL), tok(D_MODEL), tok_t, tok_t],
        out_shape=[jax.ShapeDtypeStruct((t, D_MODEL), jnp.float32),
                   jax.ShapeDtypeStruct((t, D_MODEL), jnp.float32),
                   jax.ShapeDtypeStruct((PEER_HK, t), jnp.int32),
                   jax.ShapeDtypeStruct((PEER_HK, t), jnp.float32)],
        scratch_shapes=[pltpu.VMEM((2 * PEER_HEADS, tm, PEER_HALF), BF16),
                        pltpu.VMEM((2 * PEER_HEADS, PEER_TOPK, tm), jnp.float32),
                        pltpu.VMEM((2 * PEER_HEADS, PEER_TOPK, tm), jnp.int32)],
        compiler_params=pltpu.CompilerParams(dimension_semantics=("arbitrary",),
                                             vmem_limit_bytes=48 << 20),
        name="peer_route",
    )(x2d, yag, gb, zb, w_bb.astype(BF16), w_out.astype(BF16), g_ffn.reshape(1, -1),
      w_q.astype(BF16), jnp.stack([sk1, sk2]).astype(BF16))


def _peer_kernel(final_norm, eidx_ref, xn_ref, x2_ref, gt_ref, gf_ref, uv_hbm, out_ref,
                 *scratch):
    bufs, (ytile, sem) = scratch[:PEER_SLOTS], scratch[PEER_SLOTS:]
    i = pl.program_id(0)
    n_groups = PEER_CHUNK // PEER_GROUP
    dc = D_MODEL // LANES
    lane_id = lax.broadcasted_iota(jnp.int32, (PEER_HK, LANES), 1)

    def issue_token(row, slot, j):
        for k in range(PEER_HK):
            pltpu.make_async_copy(
                uv_hbm.at[eidx_ref[row, k]],
                bufs[slot].at[:, j * PEER_HK + k, :],
                sem.at[slot]).start()

    def wait_slot(slot):
        pltpu.make_async_copy(bufs[slot], bufs[slot], sem.at[slot]).wait()

    @pl.when(i == 0)
    def _():
        for g in range(PEER_AHEAD):
            def first(j, carry):
                for k in range(PEER_HK):
                    pltpu.make_async_copy(
                        uv_hbm.at[eidx_ref[g * PEER_GROUP + j, k]],
                        bufs[g].at[:, j * PEER_HK + k, :],
                        sem.at[g]).start()
                return carry
            lax.fori_loop(0, PEER_GROUP, first, 0)

    def group(grp, slot):
        nslot = (slot + PEER_AHEAD) % PEER_SLOTS
        base = pl.multiple_of(grp * PEER_GROUP, PEER_GROUP)
        wait_slot(slot)
        xg = xn_ref[pl.ds(base, PEER_GROUP), :]

        hmat = jnp.zeros((PEER_HK, LANES), jnp.float32)
        for j in range(PEER_GROUP):
            issue_token(base + PEER_AHEAD * PEER_GROUP + j, nslot, j)
            acc = jnp.zeros((PEER_HK, LANES), jnp.float32)
            for c in range(dc):
                u = bufs[slot][c, j * PEER_HK:(j + 1) * PEER_HK, :]
                acc = acc + u * xg[j:j + 1, c * LANES:(c + 1) * LANES]
            s = jnp.sum(acc, axis=1, keepdims=True)
            hmat = jnp.where(lane_id == base + j, s, hmat)

        wmat = _gelu(hmat) * gt_ref[...]

        for j in range(PEER_GROUP):
            w = jnp.sum(jnp.where(lane_id == base + j, wmat, 0.0), axis=1, keepdims=True)
            for c in range(dc):
                v = bufs[slot][dc + c, j * PEER_HK:(j + 1) * PEER_HK, :]
                ytile[j:j + 1, c * LANES:(c + 1) * LANES] = jnp.sum(v * w, axis=0, keepdims=True)
        res = x2_ref[pl.ds(base, PEER_GROUP), :] + ytile[...]
        out_ref[pl.ds(base, PEER_GROUP), :] = _rms(res, gf_ref[...]) if final_norm else res

    def round_body(r, carry):
        for slot in range(PEER_SLOTS):
            group(r * PEER_SLOTS + slot, slot)
        return carry

    lax.fori_loop(0, n_groups // PEER_SLOTS, round_body, 0)

    @pl.when(i == pl.num_programs(0) - 1)
    def _():
        for g in range(PEER_AHEAD):
            wait_slot((n_groups + g) % PEER_SLOTS)


def _peer_experts(xn, x2, eidx, gate_t, g_final, expert_u, expert_v, final_norm):
    t = xn.shape[0]
    n_chunks = t // PEER_CHUNK
    dc = D_MODEL // LANES
    ahead_rows = PEER_AHEAD * PEER_GROUP
    e3 = eidx.reshape(n_chunks, PEER_CHUNK, PEER_HK)
    e_ext = jnp.concatenate([e3, jnp.roll(e3, -1, axis=0)[:, :ahead_rows]], axis=1)
    uv3 = jnp.concatenate([expert_u, expert_v], axis=1).reshape(-1, 2 * dc, LANES)
    buf_bytes = PEER_SLOTS * PEER_GROUP * PEER_HK * 2 * D_MODEL * 4
    tok = pl.BlockSpec((PEER_CHUNK, D_MODEL), lambda i: (i, 0))
    return pl.pallas_call(
        functools.partial(_peer_kernel, final_norm),
        grid=(n_chunks,),
        in_specs=[
            pl.BlockSpec((None, PEER_CHUNK + ahead_rows, PEER_HK), lambda i: (i, 0, 0),
                         memory_space=pltpu.SMEM),
            tok, tok,
            pl.BlockSpec((PEER_HK, PEER_CHUNK), lambda i: (0, i)),
            _resident((1, D_MODEL)),
            pl.BlockSpec(memory_space=pl.ANY),
        ],
        out_specs=tok,
        out_shape=jax.ShapeDtypeStruct((t, D_MODEL), jnp.float32),
        scratch_shapes=[
            pltpu.VMEM((2 * dc, PEER_GROUP * PEER_HK, LANES), jnp.float32)
            for _ in range(PEER_SLOTS)
        ] + [
            pltpu.VMEM((PEER_GROUP, D_MODEL), jnp.float32),
            pltpu.SemaphoreType.DMA((PEER_SLOTS,)),
        ],
        compiler_params=pltpu.CompilerParams(
            dimension_semantics=("arbitrary",),
            vmem_limit_bytes=buf_bytes + (8 << 20)),
        name="peer_experts",
    )(e_ext, xn, x2, gate_t, g_final.reshape(1, -1), uv3)


def kernel(x, g_mix, w_in, b_gate, conv_w, conv_b, lam_re, lam_im, log_dt, ssm_b_re, ssm_b_im, ssm_c_re, ssm_c_im, ssm_d, w_glu, b_glu, w_branch_a, w_branch_b, w_out, g_ffn, w_q, sub_keys_1, sub_keys_2, expert_u, expert_v, g_final):
    bsz, s, d = x.shape
    depth = g_mix.shape[0]
    x2d = x.reshape(bsz * s, d)
    for l in range(depth):
        yag, gb, su = _mix_in(x2d, s, g_mix[l], w_in[l], b_gate[l], conv_w[l], conv_b[l],
                              w_branch_a[l])
        zb = _s5(su, bsz, s, lam_re[l], lam_im[l], log_dt[l], ssm_b_re[l], ssm_b_im[l],
                 ssm_c_re[l], ssm_c_im[l], ssm_d[l].reshape(-1), w_glu[l], b_glu[l])
        x2, xn, eidx_t, gate_t = _peer_route(x2d, yag, gb, zb, w_branch_b[l], w_out[l], g_ffn[l],
                                             w_q[l], sub_keys_1[l], sub_keys_2[l])
        x2d = _peer_experts(xn, x2, eidx_t.T, gate_t, g_final, expert_u[l], expert_v[l],
                            final_norm=(l == depth - 1))
    return x2d.reshape(bsz, s, d)
```

```python
import functools

import jax
import jax.numpy as jnp
from jax import lax
from jax.experimental import pallas as pl
from jax.experimental.pallas import tpu as pltpu

D_MODEL = 1024
CONV_CH = 768
CONV_K = 3
SSM_CH = 256
SSM_GROUP = 16
SSM_GROUPS = SSM_CH // SSM_GROUP
SSM_STATE = 64
SSM_N = SSM_GROUPS * SSM_STATE
N_BRANCH = 2
IN_COLS = 3 * CONV_CH + SSM_CH + N_BRANCH * D_MODEL
PEER_HEADS = 8
PEER_KEYS = 128
PEER_QDIM = 256
PEER_HALF = PEER_QDIM // 2
PEER_TOPK = 16
PEER_HK = PEER_HEADS * PEER_TOPK
RMS_EPS = 1e-6

LANES = 128
SUBLANES = 8
BF16 = jnp.bfloat16

MIX_TM = 512
S5_TS = 256
ROUTE_TM = 256
PEER_CHUNK = 128
PEER_GROUP = SUBLANES
PEER_SLOTS = 4
PEER_AHEAD = PEER_SLOTS - 1
PEER_PITCH = 2 * D_MODEL // LANES
PEER_PAD = SUBLANES


def _gelu(x):
    return 0.5 * x * (1.0 + lax.erf(x * (2.0 ** -0.5)))


def _sigmoid(x):
    return 1.0 / (1.0 + jnp.exp(-x))


def _rms(x, g):
    return x * lax.rsqrt(jnp.mean(x * x, axis=-1, keepdims=True) + RMS_EPS) * g


def _resident(shape):
    return pl.BlockSpec(shape, lambda *_: (0,) * len(shape), pipeline_mode=pl.Buffered(1))


def _mix_in_kernel(tiles_per_seq, x_ref, g_ref, win_ref, bg_ref, cw_ref, cb_ref, wa_ref,
                   yag_ref, gb_ref, su_ref, carry):
    i = pl.program_id(0)
    tm = x_ref.shape[0]
    h = _rms(x_ref[...], g_ref[...])
    proj = jnp.dot(h.astype(BF16), win_ref[...], preferred_element_type=jnp.float32)
    c0, c1, c2, c3 = CONV_CH, 2 * CONV_CH, 3 * CONV_CH, 3 * CONV_CH + SSM_CH
    cb = proj[:, :c0]
    v = proj[:, c0:c1] * proj[:, c1:c2]
    su_ref[...] = proj[:, c2:c3]

    @pl.when(i % tiles_per_seq == 0)
    def _():
        carry[...] = jnp.zeros_like(carry)

    prev = carry[...]
    row = lax.broadcasted_iota(jnp.int32, (tm, CONV_CH), 0)
    vm1 = jnp.where(row == 0, prev[SUBLANES - 1:, :], pltpu.roll(v, 1, axis=0))
    vm2 = jnp.where(row == 0, prev[SUBLANES - 2:SUBLANES - 1, :],
                    jnp.where(row == 1, prev[SUBLANES - 1:, :], pltpu.roll(v, 2, axis=0)))
    carry[...] = v[tm - SUBLANES:, :]
    cw = cw_ref[...]
    y = cb_ref[...] + cw[0:1, :] * vm2 + cw[1:2, :] * vm1 + cw[2:3, :] * v
    ya = jnp.dot((cb * y).astype(BF16), wa_ref[...], preferred_element_type=jnp.float32)
    g = _sigmoid(proj[:, c3:] + bg_ref[...])
    yag_ref[...] = g[:, :D_MODEL] * ya
    gb_ref[...] = g[:, D_MODEL:]


def _mix_in(x2d, seq_len, g_mix, w_in, b_gate, conv_w, conv_b, w_a):
    t = x2d.shape[0]
    tm = MIX_TM
    tok = lambda n: pl.BlockSpec((tm, n), lambda i: (i, 0))
    return pl.pallas_call(
        functools.partial(_mix_in_kernel, seq_len // tm),
        grid=(t // tm,),
        in_specs=[tok(D_MODEL), _resident((1, D_MODEL)), _resident((D_MODEL, IN_COLS)),
                  _resident((1, N_BRANCH * D_MODEL)), _resident((CONV_K, CONV_CH)),
                  _resident((1, CONV_CH)), _resident((CONV_CH, D_MODEL))],
        out_specs=[tok(D_MODEL), tok(D_MODEL), tok(SSM_CH)],
        out_shape=[jax.ShapeDtypeStruct((t, D_MODEL), jnp.float32),
                   jax.ShapeDtypeStruct((t, D_MODEL), jnp.float32),
                   jax.ShapeDtypeStruct((t, SSM_CH), jnp.float32)],
        scratch_shapes=[pltpu.VMEM((SUBLANES, CONV_CH), jnp.float32)],
        compiler_params=pltpu.CompilerParams(dimension_semantics=("arbitrary",),
                                             vmem_limit_bytes=56 << 20),
        name="mix_in",
    )(x2d, g_mix.reshape(1, -1), w_in.astype(BF16), b_gate.reshape(1, -1), conv_w,
      conv_b.reshape(1, -1), w_a.astype(BF16))


def _s5_kernel(u_ref, bre_ref, bim_ref, pw_ref, cre_ref, cim_ref, d_ref, wg_ref, bgl_ref,
               z_ref, xr_scr, xi_scr, st_scr):
    j = pl.program_id(1)
    ts = u_ref.shape[0]
    u = u_ref[...]
    ub = u.astype(BF16)
    xr_scr[...] = jnp.dot(ub, bre_ref[...], preferred_element_type=jnp.float32)
    xi_scr[...] = jnp.dot(ub, bim_ref[...], preferred_element_type=jnp.float32)

    @pl.when(j == 0)
    def _():
        st_scr[...] = jnp.zeros_like(st_scr)

    pr = pw_ref[0]
    pi = pw_ref[1]
    row = lax.broadcasted_iota(jnp.int32, (SUBLANES, SSM_N), 0)

    def blk(b, carry):
        sr, si = carry
        r0 = pl.multiple_of(b * SUBLANES, SUBLANES)
        xr = xr_scr[pl.ds(r0, SUBLANES), :]
        xi = xi_scr[pl.ds(r0, SUBLANES), :]
        for sh in (1, 2, 4):
            ar = pr[sh - 1:sh, :]
            ai = pi[sh - 1:sh, :]
            rr = pltpu.roll(xr, sh, axis=0)
            ri = pltpu.roll(xi, sh, axis=0)
            keep = row >= sh
            xr, xi = (xr + jnp.where(keep, ar * rr - ai * ri, 0.0),
                      xi + jnp.where(keep, ar * ri + ai * rr, 0.0))
        xr, xi = xr + (pr * sr - pi * si), xi + (pr * si + pi * sr)
        xr_scr[pl.ds(r0, SUBLANES), :] = xr
        xi_scr[pl.ds(r0, SUBLANES), :] = xi
        return xr[SUBLANES - 1:, :], xi[SUBLANES - 1:, :]

    st = st_scr[...]
    sr, si = lax.fori_loop(0, ts // SUBLANES, blk, (st[0:1, :], st[1:2, :]))
    st_scr[0:1, :] = sr
    st_scr[1:2, :] = si
    y = (jnp.dot(xr_scr[...].astype(BF16), cre_ref[...], preferred_element_type=jnp.float32)
         - jnp.dot(xi_scr[...].astype(BF16), cim_ref[...], preferred_element_type=jnp.float32)
         + d_ref[...] * u)
    z = _gelu(y)
    gate = _sigmoid(jnp.dot(z.astype(BF16), wg_ref[...], preferred_element_type=jnp.float32)
                    + bgl_ref[...])
    z_ref[...] = z * gate


def _block_diag(m):
    g, r, c = m.shape
    eye = jnp.eye(g, dtype=m.dtype)
    return (eye[:, None, :, None] * m[:, :, None, :]).reshape(g * r, g * c)


def _s5_params(lam_re, lam_im, log_dt, b_re, b_im, c_re, c_im):
    dt = jnp.exp(log_dt)[:, None]
    mag = jnp.exp(lam_re * dt)
    ang = lam_im * dt
    a_re = mag * jnp.cos(ang)
    a_im = mag * jnp.sin(ang)
    den = lam_re * lam_re + lam_im * lam_im
    nr = a_re - 1.0
    ni = a_im
    k_re = (nr * lam_re + ni * lam_im) / den
    k_im = (ni * lam_re - nr * lam_im) / den
    bb_re = k_re[:, :, None] * b_re - k_im[:, :, None] * b_im
    bb_im = k_re[:, :, None] * b_im + k_im[:, :, None] * b_re
    bre = _block_diag(jnp.swapaxes(bb_re, 1, 2))
    bim = _block_diag(jnp.swapaxes(bb_im, 1, 2))
    cre = _block_diag(jnp.swapaxes(c_re, 1, 2))
    cim = _block_diag(jnp.swapaxes(c_im, 1, 2))
    ar = a_re.reshape(1, -1)
    ai = a_im.reshape(1, -1)
    prs, pis = [ar], [ai]
    for _ in range(SUBLANES - 1):
        prs.append(prs[-1] * ar - pis[-1] * ai)
        pis.append(prs[-2] * ai + pis[-1] * ar)
    pw = jnp.stack([jnp.concatenate(prs, 0), jnp.concatenate(pis, 0)])
    return bre, bim, pw, cre, cim


def _s5(su2d, bsz, seq_len, lam_re, lam_im, log_dt, b_re, b_im, c_re, c_im, d_skip, w_glu, b_glu):
    ts = S5_TS
    nt = seq_len // ts
    bre, bim, pw, cre, cim = _s5_params(lam_re, lam_im, log_dt, b_re, b_im, c_re, c_im)
    tile = pl.BlockSpec((ts, SSM_CH), lambda b, j: (b * nt + j, 0))
    return pl.pallas_call(
        _s5_kernel,
        grid=(bsz, nt),
        in_specs=[tile, _resident((SSM_CH, SSM_N)), _resident((SSM_CH, SSM_N)),
                  _resident((2, SUBLANES, SSM_N)), _resident((SSM_N, SSM_CH)),
                  _resident((SSM_N, SSM_CH)), _resident((1, SSM_CH)),
                  _resident((SSM_CH, SSM_CH)), _resident((1, SSM_CH))],
        out_specs=tile,
        out_shape=jax.ShapeDtypeStruct(su2d.shape, jnp.float32),
        scratch_shapes=[pltpu.VMEM((ts, SSM_N), jnp.float32), pltpu.VMEM((ts, SSM_N), jnp.float32),
                        pltpu.VMEM((SUBLANES, SSM_N), jnp.float32)],
        compiler_params=pltpu.CompilerParams(dimension_semantics=("arbitrary", "arbitrary")),
        name="s5_mixer",
    )(su2d, bre.astype(BF16), bim.astype(BF16), pw, cre.astype(BF16), cim.astype(BF16),
      d_skip.reshape(1, -1), w_glu.astype(BF16), b_glu.reshape(1, -1))


def _argmax_tiles(vals, ids):
    while len(vals) > 1:
        nv, ni = [], []
        for a in range(0, len(vals), 2):
            take = vals[a] >= vals[a + 1]
            nv.append(jnp.maximum(vals[a], vals[a + 1]))
            ni.append(jnp.where(take, ids[a], ids[a + 1]))
        vals, ids = nv, ni
    v, i = vals[0], ids[0]
    for sh in (4, 2, 1):
        v2 = pltpu.roll(v, sh, axis=0)
        i2 = pltpu.roll(i, sh, axis=0)
        take = (v > v2) | ((v == v2) & (i < i2))
        v = jnp.where(take, v, v2)
        i = jnp.where(take, i, i2)
    return v, i


def _top16_keys(s, key_ids):
    tiles = [s[v * SUBLANES:(v + 1) * SUBLANES, :] for v in range(PEER_KEYS // SUBLANES)]
    out_v, out_i = [], []
    for _ in range(PEER_TOPK):
        m, idx = _argmax_tiles(tiles, key_ids)
        out_v.append(m)
        out_i.append(idx)
        tiles = [jnp.where(k == idx, -jnp.inf, t) for t, k in zip(tiles, key_ids)]
    return out_v, out_i


_L2_TILES = (
    [(0, b) for b in range(8)],
    [(0, b) for b in range(8, 16)],
    [(1, b) for b in range(8)],
    [(2, b) for b in range(5)] + [(4, b) for b in range(3)],
    [(3, b) for b in range(4)] + [(5, 0), (5, 1), (6, 0), (6, 1)],
    [(7, 0), (7, 1)] + [(a, 0) for a in range(8, 14)],
    [(14, 0), (15, 0)],
)
assert sorted(p for t in _L2_TILES for p in t) == sorted(
    (a, b) for a in range(PEER_TOPK) for b in range(PEER_TOPK) if (a + 1) * (b + 1) <= PEER_TOPK)


def _pick_rows(sub, rows):
    out = rows[0]
    for s in range(1, len(rows)):
        if rows[s] is not rows[s - 1]:
            out = jnp.where(sub >= s, rows[s], out)
    return out


def _route_kernel(x_ref, yag_ref, gb_ref, zb_ref, wbb_ref, wout_ref, g_ref, wq_ref, sk_ref,
                  x2_ref, xn_ref, eidx_ref, gate_ref, q_scr):
    tm = x_ref.shape[0]
    yb = jnp.dot(zb_ref[...].astype(BF16), wbb_ref[...], preferred_element_type=jnp.float32)
    merged = yag_ref[...] + gb_ref[...] * yb
    x2 = x_ref[...] + jnp.dot(merged.astype(BF16), wout_ref[...],
                              preferred_element_type=jnp.float32)
    x2_ref[...] = x2
    xn = _rms(x2, g_ref[...])
    xn_ref[...] = xn
    q = jnp.dot(xn.astype(BF16), wq_ref[...], preferred_element_type=jnp.float32)
    for hh in range(2 * PEER_HEADS):
        q_scr[hh] = q[:, hh * PEER_HALF:(hh + 1) * PEER_HALF].astype(BF16)

    sub = lax.broadcasted_iota(jnp.int32, (SUBLANES, LANES), 0)
    key_ids = [sub + v * SUBLANES for v in range(PEER_KEYS // SUBLANES)]
    no_pos = PEER_TOPK * PEER_TOPK
    pos_tiles = []
    for pairs in _L2_TILES:
        p = jnp.full((SUBLANES, LANES), no_pos, jnp.int32)
        for s, (a, b) in enumerate(pairs):
            p = jnp.where(sub == s, a * PEER_TOPK + b, p)
        pos_tiles.append(p)
    pos = jnp.concatenate(pos_tiles, axis=0)

    def head(h, carry):
        s1 = lax.dot_general(sk_ref[0], q_scr[2 * h], (((1,), (1,)), ((), ())),
                             preferred_element_type=jnp.float32)
        s2 = lax.dot_general(sk_ref[1], q_scr[2 * h + 1], (((1,), (1,)), ((), ())),
                             preferred_element_type=jnp.float32)
        r0 = pl.multiple_of(h * PEER_TOPK, PEER_TOPK)
        for lh in range(tm // LANES):
            lanes = slice(lh * LANES, (lh + 1) * LANES)
            v1, i1 = _top16_keys(s1[:, lanes], key_ids)
            v2, i2 = _top16_keys(s2[:, lanes], key_ids)
            cands, cids = [], []
            for pairs in _L2_TILES:
                val = (_pick_rows(sub, [v1[a] for a, _ in pairs])
                       + _pick_rows(sub, [v2[b] for _, b in pairs]))
                if len(pairs) < SUBLANES:
                    val = jnp.where(sub < len(pairs), val, -jnp.inf)
                cands.append(val)
                cids.append(_pick_rows(sub, [i1[a] for a, _ in pairs]) * PEER_KEYS
                            + _pick_rows(sub, [i2[b] for _, b in pairs]))
            cand = jnp.concatenate(cands, axis=0)
            cid = jnp.concatenate(cids, axis=0)
            best, eids = [], []
            for _ in range(PEER_TOPK):
                m = jnp.max(cand, axis=0, keepdims=True)
                p = jnp.min(jnp.where(cand == m, pos, no_pos), axis=0, keepdims=True)
                hit = pos == p
                eids.append(jnp.max(jnp.where(hit, cid, -1), axis=0, keepdims=True))
                best.append(m)
                cand = jnp.where(hit, -jnp.inf, cand)
            best = jnp.concatenate(best, axis=0)
            ex = jnp.exp(best - best[0:1, :])
            eidx_ref[pl.ds(r0, PEER_TOPK), lanes] = jnp.concatenate(eids, axis=0)
            gate_ref[pl.ds(r0, PEER_TOPK), lanes] = ex / jnp.sum(ex, axis=0, keepdims=True)
        return carry

    lax.fori_loop(0, PEER_HEADS, head, 0)


def _peer_route(x2d, yag, gb, zb, w_bb, w_out, g_ffn, w_q, sk1, sk2):
    t = x2d.shape[0]
    tm = ROUTE_TM
    tok = lambda n: pl.BlockSpec((tm, n), lambda i: (i, 0))
    tok_t = pl.BlockSpec((PEER_HK, tm), lambda i: (0, i))
    return pl.pallas_call(
        _route_kernel,
        grid=(t // tm,),
        in_specs=[tok(D_MODEL), tok(D_MODEL), tok(D_MODEL), tok(SSM_CH),
                  _resident((SSM_CH, D_MODEL)), _resident((D_MODEL, D_MODEL)),
                  _resident((1, D_MODEL)), _resident((D_MODEL, PEER_HEADS * PEER_QDIM)),
                  _resident((2, PEER_KEYS, PEER_HALF))],
        out_specs=[tok(D_MODEL), tok(D_MODEL), tok_t, tok_t],
        out_shape=[jax.ShapeDtypeStruct((t, D_MODEL), jnp.float32),
                   jax.ShapeDtypeStruct((t, D_MODEL), jnp.float32),
                   jax.ShapeDtypeStruct((PEER_HK, t), jnp.int32),
                   jax.ShapeDtypeStruct((PEER_HK, t), jnp.float32)],
        scratch_shapes=[pltpu.VMEM((2 * PEER_HEADS, tm, PEER_HALF), BF16)],
        compiler_params=pltpu.CompilerParams(dimension_semantics=("arbitrary",),
                                             vmem_limit_bytes=48 << 20),
        name="peer_route",
    )(x2d, yag, gb, zb, w_bb.astype(BF16), w_out.astype(BF16), g_ffn.reshape(1, -1),
      w_q.astype(BF16), jnp.stack([sk1, sk2]).astype(BF16))


def _peer_kernel(final_norm, eidx_ref, xn_ref, x2_ref, gt_ref, gf_ref, uv_hbm, out_ref,
                 *scratch):
    bufs, (ytile, sem) = scratch[:PEER_SLOTS], scratch[PEER_SLOTS:]
    i = pl.program_id(0)
    n_groups = PEER_CHUNK // PEER_GROUP
    dc = D_MODEL // LANES
    lane_id = lax.broadcasted_iota(jnp.int32, (PEER_HK, LANES), 1)
    sub_id = lax.broadcasted_iota(jnp.int32, (SUBLANES, LANES), 0)

    def issue_token(row, slot, j):
        for k in range(PEER_HK):
            pltpu.make_async_copy(
                uv_hbm.at[eidx_ref[row, k]],
                bufs[slot].at[pl.ds(PEER_PAD + (j * PEER_HK + k) * PEER_PITCH, PEER_PITCH), :],
                sem.at[slot]).start(priority=k % 2)

    def wait_slot(slot):
        data = bufs[slot].at[pl.ds(PEER_PAD, PEER_GROUP * PEER_HK * PEER_PITCH), :]
        pltpu.make_async_copy(data, data, sem.at[slot]).wait()

    @pl.when(i == 0)
    def _():
        for b in bufs:
            b[0:PEER_PAD, :] = jnp.zeros((PEER_PAD, LANES), jnp.float32)
            b[b.shape[0] - PEER_PAD:, :] = jnp.zeros((PEER_PAD, LANES), jnp.float32)
        for g in range(PEER_AHEAD):
            def first(j, carry):
                for k in range(PEER_HK):
                    pltpu.make_async_copy(
                        uv_hbm.at[eidx_ref[g * PEER_GROUP + j, k]],
                        bufs[g].at[pl.ds(PEER_PAD + (j * PEER_HK + k) * PEER_PITCH, PEER_PITCH), :],
                        sem.at[g]).start(priority=k % 2)
                return carry
            lax.fori_loop(0, PEER_GROUP, first, 0)

    def group(grp, slot):
        nslot = (slot + PEER_AHEAD) % PEER_SLOTS
        base = pl.multiple_of(grp * PEER_GROUP, PEER_GROUP)
        wait_slot(slot)
        xg = xn_ref[pl.ds(base, PEER_GROUP), :]

        def skewed(row0, c):
            hi = bufs[slot][pl.ds(row0 + c, SUBLANES, stride=PEER_PITCH + 1), :]
            if c == 0:
                return hi
            lo = bufs[slot][pl.ds(row0 + c - SUBLANES, SUBLANES, stride=PEER_PITCH + 1), :]
            return jnp.where(sub_id < SUBLANES - c, hi, lo)

        hmat = jnp.zeros((PEER_HK, LANES), jnp.float32)
        for j in range(PEER_GROUP):
            issue_token(base + PEER_AHEAD * PEER_GROUP + j, nslot, j)
            x8 = jnp.concatenate([xg[j:j + 1, c * LANES:(c + 1) * LANES] for c in range(dc)], axis=0)
            xs = [x8] + [pltpu.roll(x8, SUBLANES - c, axis=0) for c in range(1, dc)]
            accs = []
            for q in range(PEER_HK // SUBLANES):
                row0 = PEER_PAD + (j * PEER_HK + q * SUBLANES) * PEER_PITCH
                acc = skewed(row0, 0) * xs[0]
                for c in range(1, dc):
                    acc = acc + skewed(row0, c) * xs[c]
                accs.append(acc)
            s = jnp.sum(jnp.concatenate(accs, axis=0), axis=1, keepdims=True)
            hmat = jnp.where(lane_id == base + j, s, hmat)

        wmat = _gelu(hmat) * gt_ref[...]

        for j in range(PEER_GROUP):
            w = jnp.sum(jnp.where(lane_id == base + j, wmat, 0.0), axis=1, keepdims=True)
            ysk = [None] * dc
            for q in range(PEER_HK // SUBLANES):
                row0 = PEER_PAD + (j * PEER_HK + q * SUBLANES) * PEER_PITCH + dc
                wq = w[q * SUBLANES:(q + 1) * SUBLANES, :]
                for c in range(dc):
                    term = skewed(row0, c) * wq
                    ysk[c] = term if q == 0 else ysk[c] + term
            y8 = ysk[0]
            for c in range(1, dc):
                y8 = y8 + pltpu.roll(ysk[c], c, axis=0)
            for c in range(dc):
                ytile[j:j + 1, c * LANES:(c + 1) * LANES] = y8[c:c + 1, :]
        res = x2_ref[pl.ds(base, PEER_GROUP), :] + ytile[...]
        out_ref[pl.ds(base, PEER_GROUP), :] = _rms(res, gf_ref[...]) if final_norm else res

    def round_body(r, carry):
        for slot in range(PEER_SLOTS):
            group(r * PEER_SLOTS + slot, slot)
        return carry

    lax.fori_loop(0, n_groups // PEER_SLOTS, round_body, 0)

    @pl.when(i == pl.num_programs(0) - 1)
    def _():
        for g in range(PEER_AHEAD):
            wait_slot((n_groups + g) % PEER_SLOTS)


def _peer_experts(xn, x2, eidx, gate_t, g_final, expert_u, expert_v, final_norm):
    t = xn.shape[0]
    n_chunks = t // PEER_CHUNK
    dc = D_MODEL // LANES
    ahead_rows = PEER_AHEAD * PEER_GROUP
    e3 = eidx.reshape(n_chunks, PEER_CHUNK, PEER_HK)
    e_ext = jnp.concatenate([e3, jnp.roll(e3, -1, axis=0)[:, :ahead_rows]], axis=1)
    uv3 = jnp.concatenate([expert_u, expert_v], axis=1).reshape(-1, 2 * dc, LANES)
    buf_rows = 2 * PEER_PAD + PEER_GROUP * PEER_HK * PEER_PITCH
    buf_bytes = PEER_SLOTS * buf_rows * LANES * 4
    tok = pl.BlockSpec((PEER_CHUNK, D_MODEL), lambda i: (i, 0))
    return pl.pallas_call(
        functools.partial(_peer_kernel, final_norm),
        grid=(n_chunks,),
        in_specs=[
            pl.BlockSpec((None, PEER_CHUNK + ahead_rows, PEER_HK), lambda i: (i, 0, 0),
                         memory_space=pltpu.SMEM),
            tok, tok,
            pl.BlockSpec((PEER_HK, PEER_CHUNK), lambda i: (0, i)),
            _resident((1, D_MODEL)),
            pl.BlockSpec(memory_space=pl.ANY),
        ],
        out_specs=tok,
        out_shape=jax.ShapeDtypeStruct((t, D_MODEL), jnp.float32),
        scratch_shapes=[
            pltpu.VMEM((buf_rows, LANES), jnp.float32)
            for _ in range(PEER_SLOTS)
        ] + [
            pltpu.VMEM((PEER_GROUP, D_MODEL), jnp.float32),
            pltpu.SemaphoreType.DMA((PEER_SLOTS,)),
        ],
        compiler_params=pltpu.CompilerParams(
            dimension_semantics=("arbitrary",),
            vmem_limit_bytes=buf_bytes + (8 << 20)),
        name="peer_experts",
    )(e_ext, xn, x2, gate_t, g_final.reshape(1, -1), uv3)


def kernel(x, g_mix, w_in, b_gate, conv_w, conv_b, lam_re, lam_im, log_dt, ssm_b_re, ssm_b_im, ssm_c_re, ssm_c_im, ssm_d, w_glu, b_glu, w_branch_a, w_branch_b, w_out, g_ffn, w_q, sub_keys_1, sub_keys_2, expert_u, expert_v, g_final):
    bsz, s, d = x.shape
    depth = g_mix.shape[0]
    x2d = x.reshape(bsz * s, d)
    for l in range(depth):
        yag, gb, su = _mix_in(x2d, s, g_mix[l], w_in[l], b_gate[l], conv_w[l], conv_b[l],
                              w_branch_a[l])
        zb = _s5(su, bsz, s, lam_re[l], lam_im[l], log_dt[l], ssm_b_re[l], ssm_b_im[l],
                 ssm_c_re[l], ssm_c_im[l], ssm_d[l].reshape(-1), w_glu[l], b_glu[l])
        x2, xn, eidx_t, gate_t = _peer_route(x2d, yag, gb, zb, w_branch_b[l], w_out[l], g_ffn[l],
                                             w_q[l], sub_keys_1[l], sub_keys_2[l])
        x2d = _peer_experts(xn, x2, eidx_t.T, gate_t, g_final, expert_u[l], expert_v[l],
                            final_norm=(l == depth - 1))
    return x2d.reshape(bsz, s, d)
```

```python
import functools

import jax
import jax.numpy as jnp
from jax import lax
from jax.experimental import pallas as pl
from jax.experimental.pallas import tpu as pltpu

D_MODEL = 1024
CONV_CH = 768
CONV_K = 3
SSM_CH = 256
SSM_GROUP = 16
SSM_GROUPS = SSM_CH // SSM_GROUP
SSM_STATE = 64
SSM_N = SSM_GROUPS * SSM_STATE
N_BRANCH = 2
IN_COLS = 3 * CONV_CH + SSM_CH + N_BRANCH * D_MODEL
PEER_HEADS = 8
PEER_KEYS = 128
PEER_QDIM = 256
PEER_HALF = PEER_QDIM // 2
PEER_TOPK = 16
PEER_HK = PEER_HEADS * PEER_TOPK
RMS_EPS = 1e-6

LANES = 128
SUBLANES = 8
BF16 = jnp.bfloat16

MIX_TM = 512
S5_TS = 256
ROUTE_TM = 256
PEER_CHUNK = 128
PEER_GROUP = SUBLANES
PEER_SLOTS = 4
PEER_AHEAD = PEER_SLOTS - 1
PEER_PITCH = D_MODEL // LANES
PEER_PAD = SUBLANES


def _gelu(x):
    return 0.5 * x * (1.0 + lax.erf(x * (2.0 ** -0.5)))


def _sigmoid(x):
    return 1.0 / (1.0 + jnp.exp(-x))


def _rms(x, g):
    return x * lax.rsqrt(jnp.mean(x * x, axis=-1, keepdims=True) + RMS_EPS) * g


def _resident(shape):
    return pl.BlockSpec(shape, lambda *_: (0,) * len(shape), pipeline_mode=pl.Buffered(1))


def _mix_in_kernel(tiles_per_seq, x_ref, g_ref, win_ref, bg_ref, cw_ref, cb_ref, wa_ref,
                   yag_ref, gb_ref, su_ref, carry):
    i = pl.program_id(0)
    tm = x_ref.shape[0]
    h = _rms(x_ref[...], g_ref[...])
    proj = jnp.dot(h.astype(BF16), win_ref[...], preferred_element_type=jnp.float32)
    c0, c1, c2, c3 = CONV_CH, 2 * CONV_CH, 3 * CONV_CH, 3 * CONV_CH + SSM_CH
    cb = proj[:, :c0]
    v = proj[:, c0:c1] * proj[:, c1:c2]
    su_ref[...] = proj[:, c2:c3]

    @pl.when(i % tiles_per_seq == 0)
    def _():
        carry[...] = jnp.zeros_like(carry)

    prev = carry[...]
    row = lax.broadcasted_iota(jnp.int32, (tm, CONV_CH), 0)
    vm1 = jnp.where(row == 0, prev[SUBLANES - 1:, :], pltpu.roll(v, 1, axis=0))
    vm2 = jnp.where(row == 0, prev[SUBLANES - 2:SUBLANES - 1, :],
                    jnp.where(row == 1, prev[SUBLANES - 1:, :], pltpu.roll(v, 2, axis=0)))
    carry[...] = v[tm - SUBLANES:, :]
    cw = cw_ref[...]
    y = cb_ref[...] + cw[0:1, :] * vm2 + cw[1:2, :] * vm1 + cw[2:3, :] * v
    ya = jnp.dot((cb * y).astype(BF16), wa_ref[...], preferred_element_type=jnp.float32)
    g = _sigmoid(proj[:, c3:] + bg_ref[...])
    yag_ref[...] = g[:, :D_MODEL] * ya
    gb_ref[...] = g[:, D_MODEL:]


def _mix_in(x2d, seq_len, g_mix, w_in, b_gate, conv_w, conv_b, w_a):
    t = x2d.shape[0]
    tm = MIX_TM
    tok = lambda n: pl.BlockSpec((tm, n), lambda i: (i, 0))
    return pl.pallas_call(
        functools.partial(_mix_in_kernel, seq_len // tm),
        grid=(t // tm,),
        in_specs=[tok(D_MODEL), _resident((1, D_MODEL)), _resident((D_MODEL, IN_COLS)),
                  _resident((1, N_BRANCH * D_MODEL)), _resident((CONV_K, CONV_CH)),
                  _resident((1, CONV_CH)), _resident((CONV_CH, D_MODEL))],
        out_specs=[tok(D_MODEL), tok(D_MODEL), tok(SSM_CH)],
        out_shape=[jax.ShapeDtypeStruct((t, D_MODEL), jnp.float32),
                   jax.ShapeDtypeStruct((t, D_MODEL), jnp.float32),
                   jax.ShapeDtypeStruct((t, SSM_CH), jnp.float32)],
        scratch_shapes=[pltpu.VMEM((SUBLANES, CONV_CH), jnp.float32)],
        compiler_params=pltpu.CompilerParams(dimension_semantics=("arbitrary",),
                                             vmem_limit_bytes=56 << 20),
        name="mix_in",
    )(x2d, g_mix.reshape(1, -1), w_in.astype(BF16), b_gate.reshape(1, -1), conv_w,
      conv_b.reshape(1, -1), w_a.astype(BF16))


def _s5_kernel(u_ref, bre_ref, bim_ref, pw_ref, cre_ref, cim_ref, d_ref, wg_ref, bgl_ref,
               z_ref, xr_scr, xi_scr, st_scr):
    j = pl.program_id(1)
    ts = u_ref.shape[0]
    u = u_ref[...]
    ub = u.astype(BF16)
    xr_scr[...] = jnp.dot(ub, bre_ref[...], preferred_element_type=jnp.float32)
    xi_scr[...] = jnp.dot(ub, bim_ref[...], preferred_element_type=jnp.float32)

    @pl.when(j == 0)
    def _():
        st_scr[...] = jnp.zeros_like(st_scr)

    pr = pw_ref[0]
    pi = pw_ref[1]
    row = lax.broadcasted_iota(jnp.int32, (SUBLANES, SSM_N), 0)

    def blk(b, carry):
        sr, si = carry
        r0 = pl.multiple_of(b * SUBLANES, SUBLANES)
        xr = xr_scr[pl.ds(r0, SUBLANES), :]
        xi = xi_scr[pl.ds(r0, SUBLANES), :]
        for sh in (1, 2, 4):
            ar = pr[sh - 1:sh, :]
            ai = pi[sh - 1:sh, :]
            rr = pltpu.roll(xr, sh, axis=0)
            ri = pltpu.roll(xi, sh, axis=0)
            keep = row >= sh
            xr, xi = (xr + jnp.where(keep, ar * rr - ai * ri, 0.0),
                      xi + jnp.where(keep, ar * ri + ai * rr, 0.0))
        xr, xi = xr + (pr * sr - pi * si), xi + (pr * si + pi * sr)
        xr_scr[pl.ds(r0, SUBLANES), :] = xr
        xi_scr[pl.ds(r0, SUBLANES), :] = xi
        return xr[SUBLANES - 1:, :], xi[SUBLANES - 1:, :]

    st = st_scr[...]
    sr, si = lax.fori_loop(0, ts // SUBLANES, blk, (st[0:1, :], st[1:2, :]))
    st_scr[0:1, :] = sr
    st_scr[1:2, :] = si
    y = (jnp.dot(xr_scr[...].astype(BF16), cre_ref[...], preferred_element_type=jnp.float32)
         - jnp.dot(xi_scr[...].astype(BF16), cim_ref[...], preferred_element_type=jnp.float32)
         + d_ref[...] * u)
    z = _gelu(y)
    gate = _sigmoid(jnp.dot(z.astype(BF16), wg_ref[...], preferred_element_type=jnp.float32)
                    + bgl_ref[...])
    z_ref[...] = z * gate


def _block_diag(m):
    g, r, c = m.shape
    eye = jnp.eye(g, dtype=m.dtype)
    return (eye[:, None, :, None] * m[:, :, None, :]).reshape(g * r, g * c)


def _s5_params(lam_re, lam_im, log_dt, b_re, b_im, c_re, c_im):
    dt = jnp.exp(log_dt)[:, None]
    mag = jnp.exp(lam_re * dt)
    ang = lam_im * dt
    a_re = mag * jnp.cos(ang)
    a_im = mag * jnp.sin(ang)
    den = lam_re * lam_re + lam_im * lam_im
    nr = a_re - 1.0
    ni = a_im
    k_re = (nr * lam_re + ni * lam_im) / den
    k_im = (ni * lam_re - nr * lam_im) / den
    bb_re = k_re[:, :, None] * b_re - k_im[:, :, None] * b_im
    bb_im = k_re[:, :, None] * b_im + k_im[:, :, None] * b_re
    bre = _block_diag(jnp.swapaxes(bb_re, 1, 2))
    bim = _block_diag(jnp.swapaxes(bb_im, 1, 2))
    cre = _block_diag(jnp.swapaxes(c_re, 1, 2))
    cim = _block_diag(jnp.swapaxes(c_im, 1, 2))
    ar = a_re.reshape(1, -1)
    ai = a_im.reshape(1, -1)
    prs, pis = [ar], [ai]
    for _ in range(SUBLANES - 1):
        prs.append(prs[-1] * ar - pis[-1] * ai)
        pis.append(prs[-2] * ai + pis[-1] * ar)
    pw = jnp.stack([jnp.concatenate(prs, 0), jnp.concatenate(pis, 0)])
    return bre, bim, pw, cre, cim


def _s5(su2d, bsz, seq_len, lam_re, lam_im, log_dt, b_re, b_im, c_re, c_im, d_skip, w_glu, b_glu):
    ts = S5_TS
    nt = seq_len // ts
    bre, bim, pw, cre, cim = _s5_params(lam_re, lam_im, log_dt, b_re, b_im, c_re, c_im)
    tile = pl.BlockSpec((ts, SSM_CH), lambda b, j: (b * nt + j, 0))
    return pl.pallas_call(
        _s5_kernel,
        grid=(bsz, nt),
        in_specs=[tile, _resident((SSM_CH, SSM_N)), _resident((SSM_CH, SSM_N)),
                  _resident((2, SUBLANES, SSM_N)), _resident((SSM_N, SSM_CH)),
                  _resident((SSM_N, SSM_CH)), _resident((1, SSM_CH)),
                  _resident((SSM_CH, SSM_CH)), _resident((1, SSM_CH))],
        out_specs=tile,
        out_shape=jax.ShapeDtypeStruct(su2d.shape, jnp.float32),
        scratch_shapes=[pltpu.VMEM((ts, SSM_N), jnp.float32), pltpu.VMEM((ts, SSM_N), jnp.float32),
                        pltpu.VMEM((SUBLANES, SSM_N), jnp.float32)],
        compiler_params=pltpu.CompilerParams(dimension_semantics=("arbitrary", "arbitrary")),
        name="s5_mixer",
    )(su2d, bre.astype(BF16), bim.astype(BF16), pw, cre.astype(BF16), cim.astype(BF16),
      d_skip.reshape(1, -1), w_glu.astype(BF16), b_glu.reshape(1, -1))


def _argmax_tiles(vals, ids):
    while len(vals) > 1:
        nv, ni = [], []
        for a in range(0, len(vals), 2):
            take = vals[a] >= vals[a + 1]
            nv.append(jnp.maximum(vals[a], vals[a + 1]))
            ni.append(jnp.where(take, ids[a], ids[a + 1]))
        vals, ids = nv, ni
    v, i = vals[0], ids[0]
    for sh in (4, 2, 1):
        v2 = pltpu.roll(v, sh, axis=0)
        i2 = pltpu.roll(i, sh, axis=0)
        take = (v > v2) | ((v == v2) & (i < i2))
        v = jnp.where(take, v, v2)
        i = jnp.where(take, i, i2)
    return v, i


def _top16_keys(s, key_ids):
    tiles = [s[v * SUBLANES:(v + 1) * SUBLANES, :] for v in range(PEER_KEYS // SUBLANES)]
    out_v, out_i = [], []
    for _ in range(PEER_TOPK):
        m, idx = _argmax_tiles(tiles, key_ids)
        out_v.append(m)
        out_i.append(idx)
        tiles = [jnp.where(k == idx, -jnp.inf, t) for t, k in zip(tiles, key_ids)]
    return out_v, out_i


_L2_TILES = (
    [(0, b) for b in range(8)],
    [(0, b) for b in range(8, 16)],
    [(1, b) for b in range(8)],
    [(2, b) for b in range(5)] + [(4, b) for b in range(3)],
    [(3, b) for b in range(4)] + [(5, 0), (5, 1), (6, 0), (6, 1)],
    [(7, 0), (7, 1)] + [(a, 0) for a in range(8, 14)],
    [(14, 0), (15, 0)],
)
assert sorted(p for t in _L2_TILES for p in t) == sorted(
    (a, b) for a in range(PEER_TOPK) for b in range(PEER_TOPK) if (a + 1) * (b + 1) <= PEER_TOPK)


def _pick_rows(sub, rows):
    out = rows[0]
    for s in range(1, len(rows)):
        if rows[s] is not rows[s - 1]:
            out = jnp.where(sub >= s, rows[s], out)
    return out


def _route_kernel(x_ref, yag_ref, gb_ref, zb_ref, wbb_ref, wout_ref, g_ref, wq_ref, sk_ref,
                  x2_ref, xn_ref, eidx_ref, gate_ref, q_scr):
    tm = x_ref.shape[0]
    yb = jnp.dot(zb_ref[...].astype(BF16), wbb_ref[...], preferred_element_type=jnp.float32)
    merged = yag_ref[...] + gb_ref[...] * yb
    x2 = x_ref[...] + jnp.dot(merged.astype(BF16), wout_ref[...],
                              preferred_element_type=jnp.float32)
    x2_ref[...] = x2
    xn = _rms(x2, g_ref[...])
    xn_ref[...] = xn
    q = jnp.dot(xn.astype(BF16), wq_ref[...], preferred_element_type=jnp.float32)
    for hh in range(2 * PEER_HEADS):
        q_scr[hh] = q[:, hh * PEER_HALF:(hh + 1) * PEER_HALF].astype(BF16)

    sub = lax.broadcasted_iota(jnp.int32, (SUBLANES, LANES), 0)
    key_ids = [sub + v * SUBLANES for v in range(PEER_KEYS // SUBLANES)]
    no_pos = PEER_TOPK * PEER_TOPK
    pos_tiles = []
    for pairs in _L2_TILES:
        p = jnp.full((SUBLANES, LANES), no_pos, jnp.int32)
        for s, (a, b) in enumerate(pairs):
            p = jnp.where(sub == s, a * PEER_TOPK + b, p)
        pos_tiles.append(p)
    pos = jnp.concatenate(pos_tiles, axis=0)

    def head(h, carry):
        s1 = lax.dot_general(sk_ref[0], q_scr[2 * h], (((1,), (1,)), ((), ())),
                             preferred_element_type=jnp.float32)
        s2 = lax.dot_general(sk_ref[1], q_scr[2 * h + 1], (((1,), (1,)), ((), ())),
                             preferred_element_type=jnp.float32)
        r0 = pl.multiple_of(h * PEER_TOPK, PEER_TOPK)
        for lh in range(tm // LANES):
            lanes = slice(lh * LANES, (lh + 1) * LANES)
            v1, i1 = _top16_keys(s1[:, lanes], key_ids)
            v2, i2 = _top16_keys(s2[:, lanes], key_ids)
            cands, cids = [], []
            for pairs in _L2_TILES:
                val = (_pick_rows(sub, [v1[a] for a, _ in pairs])
                       + _pick_rows(sub, [v2[b] for _, b in pairs]))
                if len(pairs) < SUBLANES:
                    val = jnp.where(sub < len(pairs), val, -jnp.inf)
                cands.append(val)
                cids.append(_pick_rows(sub, [i1[a] for a, _ in pairs]) * PEER_KEYS
                            + _pick_rows(sub, [i2[b] for _, b in pairs]))
            cand = jnp.concatenate(cands, axis=0)
            cid = jnp.concatenate(cids, axis=0)
            best, eids = [], []
            for _ in range(PEER_TOPK):
                m = jnp.max(cand, axis=0, keepdims=True)
                p = jnp.min(jnp.where(cand == m, pos, no_pos), axis=0, keepdims=True)
                hit = pos == p
                eids.append(jnp.max(jnp.where(hit, cid, -1), axis=0, keepdims=True))
                best.append(m)
                cand = jnp.where(hit, -jnp.inf, cand)
            best = jnp.concatenate(best, axis=0)
            ex = jnp.exp(best - best[0:1, :])
            eidx_ref[pl.ds(r0, PEER_TOPK), lanes] = jnp.concatenate(eids, axis=0)
            gate_ref[pl.ds(r0, PEER_TOPK), lanes] = ex / jnp.sum(ex, axis=0, keepdims=True)
        return carry

    lax.fori_loop(0, PEER_HEADS, head, 0)


def _peer_route(x2d, yag, gb, zb, w_bb, w_out, g_ffn, w_q, sk1, sk2):
    t = x2d.shape[0]
    tm = ROUTE_TM
    tok = lambda n: pl.BlockSpec((tm, n), lambda i: (i, 0))
    tok_t = pl.BlockSpec((PEER_HK, tm), lambda i: (0, i))
    return pl.pallas_call(
        _route_kernel,
        grid=(t // tm,),
        in_specs=[tok(D_MODEL), tok(D_MODEL), tok(D_MODEL), tok(SSM_CH),
                  _resident((SSM_CH, D_MODEL)), _resident((D_MODEL, D_MODEL)),
                  _resident((1, D_MODEL)), _resident((D_MODEL, PEER_HEADS * PEER_QDIM)),
                  _resident((2, PEER_KEYS, PEER_HALF))],
        out_specs=[tok(D_MODEL), tok(D_MODEL), tok_t, tok_t],
        out_shape=[jax.ShapeDtypeStruct((t, D_MODEL), jnp.float32),
                   jax.ShapeDtypeStruct((t, D_MODEL), jnp.float32),
                   jax.ShapeDtypeStruct((PEER_HK, t), jnp.int32),
                   jax.ShapeDtypeStruct((PEER_HK, t), jnp.float32)],
        scratch_shapes=[pltpu.VMEM((2 * PEER_HEADS, tm, PEER_HALF), BF16)],
        compiler_params=pltpu.CompilerParams(dimension_semantics=("arbitrary",),
                                             vmem_limit_bytes=48 << 20),
        name="peer_route",
    )(x2d, yag, gb, zb, w_bb.astype(BF16), w_out.astype(BF16), g_ffn.reshape(1, -1),
      w_q.astype(BF16), jnp.stack([sk1, sk2]).astype(BF16))


def _peer_kernel(final_norm, eidx_ref, xn_ref, x2_ref, gt_ref, gf_ref, uv_hbm, out_ref,
                 *scratch):
    bufs, (ytile, sem) = scratch[:PEER_SLOTS], scratch[PEER_SLOTS:]
    i = pl.program_id(0)
    n_groups = PEER_CHUNK // PEER_GROUP
    dc = D_MODEL // LANES
    lane_id = lax.broadcasted_iota(jnp.int32, (PEER_HK, LANES), 1)
    sub_id = lax.broadcasted_iota(jnp.int32, (SUBLANES, LANES), 0)

    def issue_token(row, slot, j):
        for k in range(PEER_HK):
            pltpu.make_async_copy(
                uv_hbm.at[eidx_ref[row, k]],
                bufs[slot].at[pl.ds(PEER_PAD + (j * PEER_HK + k) * PEER_PITCH, PEER_PITCH), :],
                sem.at[slot]).start(priority=k % 2)

    def wait_slot(slot):
        data = bufs[slot].at[pl.ds(PEER_PAD, PEER_GROUP * PEER_HK * PEER_PITCH), :]
        pltpu.make_async_copy(data, data, sem.at[slot]).wait()

    @pl.when(i == 0)
    def _():
        for b in bufs:
            b[0:PEER_PAD, :] = jnp.zeros((PEER_PAD, LANES), jnp.uint32)
            b[b.shape[0] - PEER_PAD:, :] = jnp.zeros((PEER_PAD, LANES), jnp.uint32)
        for g in range(PEER_AHEAD):
            def first(j, carry):
                for k in range(PEER_HK):
                    pltpu.make_async_copy(
                        uv_hbm.at[eidx_ref[g * PEER_GROUP + j, k]],
                        bufs[g].at[pl.ds(PEER_PAD + (j * PEER_HK + k) * PEER_PITCH, PEER_PITCH), :],
                        sem.at[g]).start(priority=k % 2)
                return carry
            lax.fori_loop(0, PEER_GROUP, first, 0)

    def group(grp, slot):
        nslot = (slot + PEER_AHEAD) % PEER_SLOTS
        base = pl.multiple_of(grp * PEER_GROUP, PEER_GROUP)
        wait_slot(slot)
        xg = xn_ref[pl.ds(base, PEER_GROUP), :]

        def skewed(row0, c):
            hi = bufs[slot][pl.ds(row0 + c, SUBLANES, stride=PEER_PITCH + 1), :]
            if c == 0:
                return hi
            lo = bufs[slot][pl.ds(row0 + c - SUBLANES, SUBLANES, stride=PEER_PITCH + 1), :]
            return jnp.where(sub_id < SUBLANES - c, hi, lo)

        def u_of(word):
            return lax.bitcast_convert_type(word & jnp.uint32(0xFFFF0000), jnp.float32)

        def v_of(word):
            return lax.bitcast_convert_type(word << 16, jnp.float32)

        hmat = jnp.zeros((PEER_HK, LANES), jnp.float32)
        for j in range(PEER_GROUP):
            issue_token(base + PEER_AHEAD * PEER_GROUP + j, nslot, j)
            x8 = jnp.concatenate([xg[j:j + 1, c * LANES:(c + 1) * LANES] for c in range(dc)], axis=0)
            xs = [x8] + [pltpu.roll(x8, SUBLANES - c, axis=0) for c in range(1, dc)]
            accs = []
            for q in range(PEER_HK // SUBLANES):
                row0 = PEER_PAD + (j * PEER_HK + q * SUBLANES) * PEER_PITCH
                acc = u_of(skewed(row0, 0)) * xs[0]
                for c in range(1, dc):
                    acc = acc + u_of(skewed(row0, c)) * xs[c]
                accs.append(acc)
            s = jnp.sum(jnp.concatenate(accs, axis=0), axis=1, keepdims=True)
            hmat = jnp.where(lane_id == base + j, s, hmat)

        wmat = _gelu(hmat) * gt_ref[...]

        for j in range(PEER_GROUP):
            w = jnp.sum(jnp.where(lane_id == base + j, wmat, 0.0), axis=1, keepdims=True)
            ysk = [None] * dc
            for q in range(PEER_HK // SUBLANES):
                row0 = PEER_PAD + (j * PEER_HK + q * SUBLANES) * PEER_PITCH
                wq = w[q * SUBLANES:(q + 1) * SUBLANES, :]
                for c in range(dc):
                    term = v_of(skewed(row0, c)) * wq
                    ysk[c] = term if q == 0 else ysk[c] + term
            y8 = ysk[0]
            for c in range(1, dc):
                y8 = y8 + pltpu.roll(ysk[c], c, axis=0)
            for c in range(dc):
                ytile[j:j + 1, c * LANES:(c + 1) * LANES] = y8[c:c + 1, :]
        res = x2_ref[pl.ds(base, PEER_GROUP), :] + ytile[...]
        out_ref[pl.ds(base, PEER_GROUP), :] = _rms(res, gf_ref[...]) if final_norm else res

    def round_body(r, carry):
        for slot in range(PEER_SLOTS):
            group(r * PEER_SLOTS + slot, slot)
        return carry

    lax.fori_loop(0, n_groups // PEER_SLOTS, round_body, 0)

    @pl.when(i == pl.num_programs(0) - 1)
    def _():
        for g in range(PEER_AHEAD):
            wait_slot((n_groups + g) % PEER_SLOTS)


def _peer_experts(xn, x2, eidx, gate_t, g_final, expert_u, expert_v, final_norm):
    t = xn.shape[0]
    n_chunks = t // PEER_CHUNK
    dc = D_MODEL // LANES
    ahead_rows = PEER_AHEAD * PEER_GROUP
    e3 = eidx.reshape(n_chunks, PEER_CHUNK, PEER_HK)
    e_ext = jnp.concatenate([e3, jnp.roll(e3, -1, axis=0)[:, :ahead_rows]], axis=1)
    bits = lambda w: lax.bitcast_convert_type(w.astype(BF16), jnp.uint16).astype(jnp.uint32)
    uv3 = ((bits(expert_u) << 16) | bits(expert_v)).reshape(-1, dc, LANES)
    buf_rows = 2 * PEER_PAD + PEER_GROUP * PEER_HK * PEER_PITCH
    buf_bytes = PEER_SLOTS * buf_rows * LANES * 4
    tok = pl.BlockSpec((PEER_CHUNK, D_MODEL), lambda i: (i, 0))
    return pl.pallas_call(
        functools.partial(_peer_kernel, final_norm),
        grid=(n_chunks,),
        in_specs=[
            pl.BlockSpec((None, PEER_CHUNK + ahead_rows, PEER_HK), lambda i: (i, 0, 0),
                         memory_space=pltpu.SMEM),
            tok, tok,
            pl.BlockSpec((PEER_HK, PEER_CHUNK), lambda i: (0, i)),
            _resident((1, D_MODEL)),
            pl.BlockSpec(memory_space=pl.ANY),
        ],
        out_specs=tok,
        out_shape=jax.ShapeDtypeStruct((t, D_MODEL), jnp.float32),
        scratch_shapes=[
            pltpu.VMEM((buf_rows, LANES), jnp.uint32)
            for _ in range(PEER_SLOTS)
        ] + [
            pltpu.VMEM((PEER_GROUP, D_MODEL), jnp.float32),
            pltpu.SemaphoreType.DMA((PEER_SLOTS,)),
        ],
        compiler_params=pltpu.CompilerParams(
            dimension_semantics=("arbitrary",),
            vmem_limit_bytes=buf_bytes + (8 << 20)),
        name="peer_experts",
    )(e_ext, xn, x2, gate_t, g_final.reshape(1, -1), uv3)


def kernel(x, g_mix, w_in, b_gate, conv_w, conv_b, lam_re, lam_im, log_dt, ssm_b_re, ssm_b_im, ssm_c_re, ssm_c_im, ssm_d, w_glu, b_glu, w_branch_a, w_branch_b, w_out, g_ffn, w_q, sub_keys_1, sub_keys_2, expert_u, expert_v, g_final):
    bsz, s, d = x.shape
    depth = g_mix.shape[0]
    x2d = x.reshape(bsz * s, d)
    for l in range(depth):
        yag, gb, su = _mix_in(x2d, s, g_mix[l], w_in[l], b_gate[l], conv_w[l], conv_b[l],
                              w_branch_a[l])
        zb = _s5(su, bsz, s, lam_re[l], lam_im[l], log_dt[l], ssm_b_re[l], ssm_b_im[l],
                 ssm_c_re[l], ssm_c_im[l], ssm_d[l].reshape(-1), w_glu[l], b_glu[l])
        x2, xn, eidx_t, gate_t = _peer_route(x2d, yag, gb, zb, w_branch_b[l], w_out[l], g_ffn[l],
                                             w_q[l], sub_keys_1[l], sub_keys_2[l])
        x2d = _peer_experts(xn, x2, eidx_t.T, gate_t, g_final, expert_u[l], expert_v[l],
                            final_norm=(l == depth - 1))
    return x2d.reshape(bsz, s, d)
```
